```python
import math
import jax, jax.numpy as jnp
from jax import lax
import numpy as np

D_MODEL = 2048
BATCH = 16
SEQ = 2048
DEPTH = 1
DEC_BATCH = 8
DEC_SEQ = 4096
PAST_LEN = 128

DA_HEADS = 8
DA_DK = 64
DA_DV = 2 * DA_DK
SW_Q_HEADS = 16
SW_KV_HEADS = 4
SW_GROUP = SW_Q_HEADS // SW_KV_HEADS
SW_DH = 64
WINDOW = 128
BLOCK = 128
N_BUCKETS = 32
MAX_DISTANCE = 128
N_BIAS_HEADS = DA_HEADS + SW_Q_HEADS
D_FF = 5632
EPS = 1e-6

DA_Q = DA_HEADS * 2 * DA_DK
DA_K = DA_HEADS * 2 * DA_DK
DA_V = DA_HEADS * DA_DV
SW_Q = SW_Q_HEADS * SW_DH
SW_K = SW_KV_HEADS * SW_DH
SW_V = SW_KV_HEADS * SW_DH
D_IN = DA_Q + DA_K + DA_V + SW_Q + SW_K + SW_V
SPLITS = list(np.cumsum([DA_Q, DA_K, DA_V, SW_Q, SW_K]))
D_MIX = DA_HEADS * DA_DV + SW_Q_HEADS * SW_DH

kernel_name = "hybrid_diffattn_swa_macaron_encoder"


def rmsnorm(x, g):
    xf = x.astype(jnp.float32)
    y = xf * lax.rsqrt(jnp.mean(xf * xf, axis=-1, keepdims=True) + EPS)
    return (y * g.astype(jnp.float32)).astype(x.dtype)


def swiglu(x, w_gu, w_down):
    gu = x @ w_gu
    g, u = jnp.split(gu, 2, axis=-1)
    return (jax.nn.silu(g) * u) @ w_down


def t5_bucket(rp):
    half = N_BUCKETS // 2
    max_exact = half // 2
    ret = jnp.where(rp > 0, half, 0)
    n = jnp.abs(rp)
    nf = jnp.maximum(n, 1).astype(jnp.float32)
    large = max_exact + (jnp.log(nf / max_exact) / math.log(MAX_DISTANCE / max_exact)
                         * (half - max_exact)).astype(jnp.int32)
    large = jnp.minimum(large, half - 1)
    return ret + jnp.where(n < max_exact, n, large)


def diff_attention(q, k, v, lam, subln_g, lambda_init, rel_bias):
    B, S = q.shape[0], q.shape[1]
    nb = S // BLOCK
    scale = DA_DK ** -0.5
    qb = q.reshape(B, nb, BLOCK, DA_HEADS, 2, DA_DK).transpose(1, 0, 3, 4, 2, 5)
    kpos = jnp.arange(S)
    bias_tab = rel_bias[:, :DA_HEADS]

    def block(args):
        qi, i = args
        qpos = i * BLOCK + jnp.arange(BLOCK)
        bias = bias_tab[t5_bucket(kpos[None, :] - qpos[:, None])].astype(jnp.float32)
        s = jnp.einsum('bhmqd,bkhmd->bhmqk', qi, k).astype(jnp.float32) * scale \
            + bias.transpose(2, 0, 1)[None, :, None]
        p = jax.nn.softmax(s, axis=-1)
        a = p[:, :, 0] - lam * p[:, :, 1]
        return jnp.einsum('bhqk,bkhd->bqhd', a.astype(v.dtype), v)

    o = lax.map(block, (qb, jnp.arange(nb)))
    o = o.transpose(1, 0, 2, 3, 4).reshape(B, S, DA_HEADS, DA_DV)
    o = rmsnorm(o, subln_g) * (1.0 - lambda_init)
    return o.reshape(B, S, DA_HEADS * DA_DV)


def window_attention(q, k, v, sink, rel_bias):
    B, S = q.shape[0], q.shape[1]
    nb = S // BLOCK
    scale = SW_DH ** -0.5
    qb = jnp.moveaxis(q.reshape(B, nb, BLOCK, SW_KV_HEADS, SW_GROUP, SW_DH), 1, 0)

    def band(t):
        tp = jnp.pad(t, ((0, 0), (BLOCK, BLOCK), (0, 0), (0, 0)))
        tp = tp.reshape(B, nb + 2, BLOCK, SW_KV_HEADS, SW_DH)
        tb = jnp.concatenate([tp[:, :-2], tp[:, 1:-1], tp[:, 2:]], axis=2)
        return jnp.moveaxis(tb, 1, 0)

    kb, vb = band(k), band(v)
    off = jnp.arange(3 * BLOCK) - BLOCK
    rp = off[None, :] - jnp.arange(BLOCK)[:, None]
    kpos = jnp.arange(nb)[:, None] * BLOCK + off[None, :]
    valid = (jnp.abs(rp) <= WINDOW)[None] & ((kpos >= 0) & (kpos < S))[:, None, :]
    bias = rel_bias[t5_bucket(rp)][..., DA_HEADS:].astype(jnp.float32)
    bias = bias.transpose(2, 0, 1).reshape(SW_KV_HEADS, SW_GROUP, BLOCK, 3 * BLOCK)
    sk = sink.astype(jnp.float32).reshape(SW_KV_HEADS, SW_GROUP)[:, :, None, None]

    def block(args):
        qi, ki, vi, ok = args
        s = jnp.einsum('bqhgd,bkhd->bhgqk', qi, ki).astype(jnp.float32) * scale + bias
        s = jnp.where(ok[None, None, None], s, -jnp.inf)
        m = jnp.maximum(jnp.max(s, axis=-1, keepdims=True), sk)
        e = jnp.exp(s - m)
        p = e / (jnp.sum(e, axis=-1, keepdims=True) + jnp.exp(sk - m))
        return jnp.einsum('bhgqk,bkhd->bqhgd', p.astype(vi.dtype), vi)

    o = lax.map(block, (qb, kb, vb, valid))
    return jnp.moveaxis(o, 0, 1).reshape(B, S, SW_Q_HEADS * SW_DH)


def encoder_layer(x, l, rel_bias, g_ffn1_pre, w_ffn1_gu, w_ffn1_down, g_ffn1_post,
                  g_mix_pre, w_in, lambda_q1, lambda_k1, lambda_q2, lambda_k2,
                  g_diff_subln, sink_logit, w_out, g_mix_post,
                  g_ffn2_pre, w_ffn2_gu, w_ffn2_down, g_ffn2_post):
    B, S = x.shape[0], x.shape[1]
    lambda_init = 0.8 - 0.6 * math.exp(-0.3 * l)
    h = x + 0.5 * rmsnorm(swiglu(rmsnorm(x, g_ffn1_pre), w_ffn1_gu, w_ffn1_down), g_ffn1_post)
    n = rmsnorm(h, g_mix_pre)
    proj = n @ w_in
    q_da, k_da, v_da, q_sw, k_sw, v_sw = jnp.split(proj, SPLITS, axis=-1)
    lam = (jnp.exp(jnp.sum(lambda_q1.astype(jnp.float32) * lambda_k1.astype(jnp.float32)))
           - jnp.exp(jnp.sum(lambda_q2.astype(jnp.float32) * lambda_k2.astype(jnp.float32)))
           + lambda_init)
    o_da = diff_attention(q_da.reshape(B, S, DA_HEADS, 2, DA_DK),
                          k_da.reshape(B, S, DA_HEADS, 2, DA_DK),
                          v_da.reshape(B, S, DA_HEADS, DA_DV),
                          lam, g_diff_subln, lambda_init, rel_bias)
    o_sw = window_attention(q_sw.reshape(B, S, SW_Q_HEADS, SW_DH),
                            k_sw.reshape(B, S, SW_KV_HEADS, SW_DH),
                            v_sw.reshape(B, S, SW_KV_HEADS, SW_DH),
                            sink_logit, rel_bias)
    mix = jnp.concatenate([o_da, o_sw], axis=-1) @ w_out
    h = h + rmsnorm(mix, g_mix_post)
    return h + 0.5 * rmsnorm(swiglu(rmsnorm(h, g_ffn2_pre), w_ffn2_gu, w_ffn2_down), g_ffn2_post)


def setup_inputs(seed: int = 0) -> dict:
    key = jax.random.key(seed)
    ks = jax.random.split(key, 24)
    f32 = jnp.float32

    def nrm(k, shape, s):
        return jax.random.normal(k, shape, f32) * s

    def gain(k, shape):
        return 1.0 + 0.02 * jax.random.normal(k, shape, f32)

    return {
        "x_prompt": nrm(ks[0], (BATCH, SEQ, D_MODEL), 1.0),
        "x_sample": nrm(ks[1], (DEC_BATCH, DEC_SEQ, D_MODEL), 1.0),
        "rel_bias": nrm(ks[2], (N_BUCKETS, N_BIAS_HEADS), 0.2),
        "g_ffn1_pre": gain(ks[3], (DEPTH, D_MODEL)),
        "w_ffn1_gu": nrm(ks[4], (DEPTH, D_MODEL, 2 * D_FF), D_MODEL ** -0.5),
        "w_ffn1_down": nrm(ks[5], (DEPTH, D_FF, D_MODEL), D_FF ** -0.5),
        "g_ffn1_post": gain(ks[6], (DEPTH, D_MODEL)),
        "g_mix_pre": gain(ks[7], (DEPTH, D_MODEL)),
        "w_in": nrm(ks[8], (DEPTH, D_MODEL, D_IN), D_MODEL ** -0.5),
        "lambda_q1": nrm(ks[9], (DEPTH, DA_DK), 0.1),
        "lambda_k1": nrm(ks[10], (DEPTH, DA_DK), 0.1),
        "lambda_q2": nrm(ks[11], (DEPTH, DA_DK), 0.1),
        "lambda_k2": nrm(ks[12], (DEPTH, DA_DK), 0.1),
        "g_diff_subln": gain(ks[13], (DEPTH, DA_DV)),
        "sink_logit": nrm(ks[14], (DEPTH, SW_Q_HEADS), 0.5),
        "w_out": nrm(ks[15], (DEPTH, D_MIX, D_MODEL), D_MIX ** -0.5),
        "g_mix_post": gain(ks[16], (DEPTH, D_MODEL)),
        "g_ffn2_pre": gain(ks[17], (DEPTH, D_MODEL)),
        "w_ffn2_gu": nrm(ks[18], (DEPTH, D_MODEL, 2 * D_FF), D_MODEL ** -0.5),
        "w_ffn2_down": nrm(ks[19], (DEPTH, D_FF, D_MODEL), D_FF ** -0.5),
        "g_ffn2_post": gain(ks[20], (DEPTH, D_MODEL)),
    }


def reference(x_prompt, x_sample, rel_bias, g_ffn1_pre, w_ffn1_gu, w_ffn1_down, g_ffn1_post,
              g_mix_pre, w_in, lambda_q1, lambda_k1, lambda_q2, lambda_k2,
              g_diff_subln, sink_logit, w_out, g_mix_post,
              g_ffn2_pre, w_ffn2_gu, w_ffn2_down, g_ffn2_post):
    def trunk(x):
        for l in range(DEPTH):
            x = encoder_layer(x, l, rel_bias, g_ffn1_pre[l], w_ffn1_gu[l], w_ffn1_down[l],
                              g_ffn1_post[l], g_mix_pre[l], w_in[l], lambda_q1[l], lambda_k1[l],
                              lambda_q2[l], lambda_k2[l], g_diff_subln[l], sink_logit[l],
                              w_out[l], g_mix_post[l], g_ffn2_pre[l], w_ffn2_gu[l],
                              w_ffn2_down[l], g_ffn2_post[l])
        return x
    y_prompt = trunk(x_prompt)
    y_sample = trunk(x_sample)
    return (y_prompt, y_sample)
```

```python
import functools
import math

import jax
import jax.numpy as jnp
from jax import lax
from jax.experimental import pallas as pl
from jax.experimental.pallas import tpu as pltpu

F32 = jnp.float32
BF16 = jnp.bfloat16

D_MODEL = 2048
D_FF = 5632
DA_HEADS = 8
DA_DK = 64
DA_DV = 128
SW_Q_HEADS = 16
SW_KV_HEADS = 4
SW_DH = 64
WINDOW = 128
N_BUCKETS = 32
MAX_DISTANCE = 128
EPS = 1e-6
D_IN = 4608
LANES = 128
N_SLABS = D_IN // LANES
QK_SCALE = 0.125
NEG_BIG = -1e30

FFN_TM = 512
FFN_TF = 512
PROJ_TM = 512
PROJ_TN = 512
DA_TQ = 256
DA_CK = 512
SW_TQ = 256
VMEM_LIMIT = 56 * 1024 * 1024


def _rms(x, g):
    return x * lax.rsqrt(jnp.mean(x * x, axis=-1, keepdims=True) + EPS) * g


def _ffn_kernel(x_ref, gpre_ref, wg_ref, wu_ref, wd_ref, gpost_ref, o_ref, xn_ref):
    j = pl.program_id(1)

    @pl.when(j == 0)
    def _():
        xn_ref[...] = _rms(x_ref[...], gpre_ref[...]).astype(BF16)

    xn = xn_ref[...]
    g = jnp.dot(xn, wg_ref[...], preferred_element_type=F32)
    u = jnp.dot(xn, wu_ref[...], preferred_element_type=F32)
    a = (g / (1.0 + jnp.exp(-g)) * u).astype(BF16)
    part = jnp.dot(a, wd_ref[...], preferred_element_type=F32)

    @pl.when(j == 0)
    def _():
        o_ref[...] = part

    @pl.when(j > 0)
    def _():
        o_ref[...] += part

    @pl.when(j == pl.num_programs(1) - 1)
    def _():
        o_ref[...] = x_ref[...] + 0.5 * _rms(o_ref[...], gpost_ref[...])


def _ffn(x, g_pre, w_gu, w_down, g_post):
    T = x.shape[0]
    nf = D_FF // FFN_TF
    return pl.pallas_call(
        _ffn_kernel,
        grid=(T // FFN_TM, nf),
        in_specs=[
            pl.BlockSpec((FFN_TM, D_MODEL), lambda i, j: (i, 0)),
            pl.BlockSpec((1, D_MODEL), lambda i, j: (0, 0)),
            pl.BlockSpec((D_MODEL, FFN_TF), lambda i, j: (0, j)),
            pl.BlockSpec((D_MODEL, FFN_TF), lambda i, j: (0, j + nf)),
            pl.BlockSpec((FFN_TF, D_MODEL), lambda i, j: (j, 0)),
            pl.BlockSpec((1, D_MODEL), lambda i, j: (0, 0)),
        ],
        out_specs=pl.BlockSpec((FFN_TM, D_MODEL), lambda i, j: (i, 0)),
        out_shape=jax.ShapeDtypeStruct((T, D_MODEL), F32),
        scratch_shapes=[pltpu.VMEM((FFN_TM, D_MODEL), BF16)],
        compiler_params=pltpu.CompilerParams(
            dimension_semantics=("parallel", "arbitrary"),
            vmem_limit_bytes=VMEM_LIMIT),
        name="ffn",
    )(x, g_pre, w_gu, w_gu, w_down, g_post)


def _proj_kernel(h_ref, g_ref, w_ref, o_ref):
    n = _rms(h_ref[...], g_ref[...]).astype(BF16)
    per = PROJ_TN // LANES
    for c in range(D_IN // PROJ_TN):
        r = jnp.dot(n, w_ref[:, c * PROJ_TN:(c + 1) * PROJ_TN], preferred_element_type=F32)
        for s in range(per):
            slab = c * per + s
            blk = r[:, s * LANES:(s + 1) * LANES]
            if slab < 8 or 24 <= slab < 32:
                blk = blk * QK_SCALE
            o_ref[slab] = blk.astype(BF16)


def _in_proj(h, g, w_in):
    T = h.shape[0]
    return pl.pallas_call(
        _proj_kernel,
        grid=(T // PROJ_TM,),
        in_specs=[
            pl.BlockSpec((PROJ_TM, D_MODEL), lambda i: (i, 0)),
            pl.BlockSpec((1, D_MODEL), lambda i: (0, 0)),
            pl.BlockSpec((D_MODEL, D_IN), lambda i: (0, 0), pipeline_mode=pl.Buffered(1)),
        ],
        out_specs=pl.BlockSpec((N_SLABS, PROJ_TM, LANES), lambda i: (0, i, 0)),
        out_shape=jax.ShapeDtypeStruct((N_SLABS, T, LANES), BF16),
        compiler_params=pltpu.CompilerParams(
            dimension_semantics=("parallel",),
            vmem_limit_bytes=VMEM_LIMIT),
        name="in_proj",
    )(h, g, w_in)


def _t5_bucket(rp):
    half = N_BUCKETS // 2
    max_exact = half // 2
    ret = jnp.where(rp > 0, half, 0)
    n = jnp.abs(rp)
    nf = jnp.maximum(n, 1).astype(jnp.float32)
    large = max_exact + (jnp.log(nf / max_exact) / math.log(MAX_DISTANCE / max_exact)
                         * (half - max_exact)).astype(jnp.int32)
    large = jnp.minimum(large, half - 1)
    return ret + jnp.where(n < max_exact, n, large)


def _da_bias_shift(ck):
    return ck + WINDOW


def _da_bias_table(rel_bias, tq, ck):
    width = 2 * ck + tq + 2 * WINDOW
    rp = jnp.arange(width)[None, :] - jnp.arange(tq)[:, None] - _da_bias_shift(ck)
    tab = rel_bias[:, :DA_HEADS][_t5_bucket(rp)].astype(F32)
    tab = tab.transpose(2, 0, 1).reshape(DA_HEADS, tq, width // LANES, LANES)
    return tab.transpose(0, 2, 1, 3)


def _sw_bias_table(rel_bias, tq):
    wk = tq + 2 * WINDOW
    starts = jnp.array([0, -WINDOW, -2 * WINDOW])
    rp = starts[:, None, None] + jnp.arange(wk)[None, None, :] - jnp.arange(tq)[None, :, None]
    tab = rel_bias[:, DA_HEADS:][_t5_bucket(rp)].astype(F32)
    tab = jnp.where((jnp.abs(rp) <= WINDOW)[..., None], tab, NEG_BIG)
    return tab.transpose(3, 0, 1, 2)


def _da_kernel(lq1_ref, lk1_ref, lq2_ref, lk2_ref, q_ref, k_ref, v_ref, bias_ref, g_ref,
               o_ref, s_ref, *, s_len, lambda_init):
    tq, ck = DA_TQ, DA_CK
    nck = s_len // ck
    per = ck // LANES
    q0 = pl.program_id(2) * tq

    q = q_ref[0]
    lo = lax.broadcasted_iota(jnp.int32, (tq, LANES), 1) < DA_DK
    zero = jnp.zeros_like(q)
    qs = jnp.concatenate([jnp.where(lo, q, zero), jnp.where(lo, zero, q)], axis=0)

    dmin = -_da_bias_shift(ck)
    dmax = tq + WINDOW
    m = jnp.full((2 * tq, 1), -jnp.inf, F32)
    for c in range(nck):
        kc = k_ref[0, c * ck:(c + 1) * ck, :]
        s = lax.dot_general(qs, kc, (((1,), (1,)), ((), ())), preferred_element_type=F32)
        ob = (jnp.clip(c * ck - q0, dmin, dmax) - dmin) // LANES
        bias = jnp.concatenate([bias_ref[0, ob + jb] for jb in range(per)], axis=1)
        s = s + jnp.concatenate([bias, bias], axis=0)
        s_ref[:, c * ck:(c + 1) * ck] = s
        m = jnp.maximum(m, jnp.max(s, axis=1, keepdims=True))

    l = jnp.zeros((2 * tq, 1), F32)
    acc = jnp.zeros((2 * tq, DA_DV), F32)
    for c in range(nck):
        p = jnp.exp(s_ref[:, c * ck:(c + 1) * ck] - m)
        l = l + jnp.sum(p, axis=1, keepdims=True)
        acc = acc + jnp.dot(p.astype(BF16), v_ref[0, c * ck:(c + 1) * ck, :],
                            preferred_element_type=F32)

    lam = (jnp.exp(jnp.sum(lq1_ref[...] * lk1_ref[...], axis=1, keepdims=True))
           - jnp.exp(jnp.sum(lq2_ref[...] * lk2_ref[...], axis=1, keepdims=True))
           + lambda_init)
    o = acc / l
    o = o[:tq] - lam * o[tq:]
    o_ref[0] = (_rms(o, g_ref[...]) * (1.0 - lambda_init)).astype(BF16)


def _diff_attn(proj, bias_tab, lq1, lk1, lq2, lk2, g_subln, batch, s_len, lambda_init):
    T = batch * s_len
    nt = s_len // DA_TQ
    nblk = bias_tab.shape[1]
    vec = pl.BlockSpec((1, DA_DK), lambda h, b, t: (0, 0))
    return pl.pallas_call(
        functools.partial(_da_kernel, s_len=s_len, lambda_init=lambda_init),
        grid=(DA_HEADS, batch, nt),
        in_specs=[
            vec, vec, vec, vec,
            pl.BlockSpec((1, DA_TQ, LANES), lambda h, b, t: (h, b * nt + t, 0)),
            pl.BlockSpec((1, s_len, LANES), lambda h, b, t: (8 + h, b, 0)),
            pl.BlockSpec((1, s_len, LANES), lambda h, b, t: (16 + h, b, 0)),
            pl.BlockSpec((1, nblk, DA_TQ, LANES), lambda h, b, t: (h, 0, 0, 0)),
            pl.BlockSpec((1, DA_DV), lambda h, b, t: (0, 0)),
        ],
        out_specs=pl.BlockSpec((1, DA_TQ, LANES), lambda h, b, t: (h, b * nt + t, 0)),
        out_shape=jax.ShapeDtypeStruct((DA_HEADS, T, LANES), BF16),
        scratch_shapes=[pltpu.VMEM((2 * DA_TQ, s_len), F32)],
        compiler_params=pltpu.CompilerParams(
            dimension_semantics=("parallel", "parallel", "parallel"),
            vmem_limit_bytes=VMEM_LIMIT),
        name="diff_attn",
    )(lq1, lk1, lq2, lk2, proj, proj, proj, bias_tab, g_subln)


def _swap_halves(x):
    return jnp.concatenate([x[:, SW_DH:], x[:, :SW_DH]], axis=1)


def _sw_kernel(sink_ref, q_ref, k_ref, v_ref, bias_ref, o_ref, *, s_len):
    tq = SW_TQ
    wk = tq + 2 * WINDOW
    p = pl.program_id(0)
    q0 = pl.program_id(2) * tq
    start = pl.multiple_of(jnp.clip(q0 - WINDOW, 0, s_len - wk), LANES)
    kw = k_ref[0, pl.ds(start, wk), :]
    vw = v_ref[0, pl.ds(start, wk), :]
    lo = lax.broadcasted_iota(jnp.int32, (tq, LANES), 1) < SW_DH

    outs = []
    for j in range(8):
        kv_half, q_half = j // 4, j % 2
        xq = q_ref[j // 2]
        if q_half != kv_half:
            xq = _swap_halves(xq)
        zero = jnp.zeros_like(xq)
        xq = jnp.where(lo, xq, zero) if kv_half == 0 else jnp.where(lo, zero, xq)
        s = lax.dot_general(xq, kw, (((1,), (1,)), ((), ())), preferred_element_type=F32)
        s = s + bias_ref[j, 0]
        sk = sink_ref[p * 8 + j]
        m = jnp.maximum(jnp.max(s, axis=1, keepdims=True), sk)
        e = jnp.exp(s - m)
        den = jnp.sum(e, axis=1, keepdims=True) + jnp.exp(sk - m)
        o = jnp.dot(e.astype(BF16), vw, preferred_element_type=F32) / den
        if q_half != kv_half:
            o = _swap_halves(o)
        outs.append(o)
    for ob in range(4):
        o_ref[ob] = jnp.where(lo, outs[2 * ob], outs[2 * ob + 1]).astype(BF16)


def _win_attn(proj, bias_tab, sink, batch, s_len):
    T = batch * s_len
    nt = s_len // SW_TQ
    wk = SW_TQ + 2 * WINDOW

    def case(t):
        return jnp.where(t == 0, 0, jnp.where(t == nt - 1, 2, 1))

    return pl.pallas_call(
        functools.partial(_sw_kernel, s_len=s_len),
        grid=(2, batch, nt),
        in_specs=[
            pl.BlockSpec(memory_space=pltpu.SMEM),
            pl.BlockSpec((4, SW_TQ, LANES), lambda p, b, t: (6 + p, b * nt + t, 0)),
            pl.BlockSpec((1, s_len, LANES), lambda p, b, t: (32 + p, b, 0)),
            pl.BlockSpec((1, s_len, LANES), lambda p, b, t: (34 + p, b, 0)),
            pl.BlockSpec((8, 1, SW_TQ, wk), lambda p, b, t: (p, case(t), 0, 0)),
        ],
        out_specs=pl.BlockSpec((4, SW_TQ, LANES), lambda p, b, t: (p, b * nt + t, 0)),
        out_shape=jax.ShapeDtypeStruct((SW_Q_HEADS // 2, T, LANES), BF16),
        compiler_params=pltpu.CompilerParams(
            dimension_semantics=("parallel", "parallel", "parallel"),
            vmem_limit_bytes=VMEM_LIMIT),
        name="win_attn",
    )(sink, proj, proj, proj, bias_tab)


def _out_kernel(da_ref, sw_ref, h_ref, w_ref, g_ref, o_ref):
    a = jnp.concatenate([da_ref[c] for c in range(8)] + [sw_ref[c] for c in range(8)], axis=1)
    mix = jnp.dot(a, w_ref[...], preferred_element_type=F32)
    o_ref[...] = h_ref[...] + _rms(mix, g_ref[...])


def _out_proj(o_da, o_sw, h, w_out, g):
    T = h.shape[0]
    tm = PROJ_TM
    return pl.pallas_call(
        _out_kernel,
        grid=(T // tm,),
        in_specs=[
            pl.BlockSpec((8, tm, LANES), lambda i: (0, i, 0)),
            pl.BlockSpec((8, tm, LANES), lambda i: (0, i, 0)),
            pl.BlockSpec((tm, D_MODEL), lambda i: (i, 0)),
            pl.BlockSpec((D_MODEL, D_MODEL), lambda i: (0, 0), pipeline_mode=pl.Buffered(1)),
            pl.BlockSpec((1, D_MODEL), lambda i: (0, 0)),
        ],
        out_specs=pl.BlockSpec((tm, D_MODEL), lambda i: (i, 0)),
        out_shape=jax.ShapeDtypeStruct((T, D_MODEL), F32),
        compiler_params=pltpu.CompilerParams(
            dimension_semantics=("parallel",),
            vmem_limit_bytes=VMEM_LIMIT),
        name="out_proj",
    )(o_da, o_sw, h, w_out, g)


def _layer(x, l, wts):
    batch, s_len, _ = x.shape
    assert s_len % DA_TQ == 0 and s_len % DA_CK == 0 and s_len % SW_TQ == 0
    assert s_len >= SW_TQ + 2 * WINDOW and (batch * s_len) % FFN_TM == 0
    lambda_init = 0.8 - 0.6 * math.exp(-0.3 * l)
    x2 = x.reshape(batch * s_len, D_MODEL)
    h = _ffn(x2, wts["g_ffn1_pre"], wts["w_ffn1_gu"], wts["w_ffn1_down"], wts["g_ffn1_post"])
    proj = _in_proj(h, wts["g_mix_pre"], wts["w_in"])
    o_da = _diff_attn(proj, wts["da_bias"], wts["lambda_q1"], wts["lambda_k1"], wts["lambda_q2"],
                      wts["lambda_k2"], wts["g_diff_subln"], batch, s_len, lambda_init)
    o_sw = _win_attn(proj, wts["sw_bias"], wts["sink_logit"], batch, s_len)
    h2 = _out_proj(o_da, o_sw, h, wts["w_out"], wts["g_mix_post"])
    y = _ffn(h2, wts["g_ffn2_pre"], wts["w_ffn2_gu"], wts["w_ffn2_down"], wts["g_ffn2_post"])
    return y.reshape(batch, s_len, D_MODEL)


def kernel(x_prompt, x_sample, rel_bias, g_ffn1_pre, w_ffn1_gu, w_ffn1_down, g_ffn1_post, g_mix_pre, w_in, lambda_q1, lambda_k1, lambda_q2, lambda_k2, g_diff_subln, sink_logit, w_out, g_mix_post, g_ffn2_pre, w_ffn2_gu, w_ffn2_down, g_ffn2_post):
    depth = w_in.shape[0]
    layers = []
    for l in range(depth):
        layers.append({
            "g_ffn1_pre": g_ffn1_pre[l][None], "g_ffn1_post": g_ffn1_post[l][None],
            "w_ffn1_gu": w_ffn1_gu[l].astype(BF16), "w_ffn1_down": w_ffn1_down[l].astype(BF16),
            "g_mix_pre": g_mix_pre[l][None], "w_in": w_in[l].astype(BF16),
            "lambda_q1": lambda_q1[l][None], "lambda_k1": lambda_k1[l][None],
            "lambda_q2": lambda_q2[l][None], "lambda_k2": lambda_k2[l][None],
            "g_diff_subln": g_diff_subln[l][None], "sink_logit": sink_logit[l],
            "w_out": w_out[l].astype(BF16), "g_mix_post": g_mix_post[l][None],
            "g_ffn2_pre": g_ffn2_pre[l][None], "g_ffn2_post": g_ffn2_post[l][None],
            "w_ffn2_gu": w_ffn2_gu[l].astype(BF16), "w_ffn2_down": w_ffn2_down[l].astype(BF16),
            "da_bias": _da_bias_table(rel_bias, DA_TQ, DA_CK),
            "sw_bias": _sw_bias_table(rel_bias, SW_TQ),
        })

    def trunk(x):
        for l in range(depth):
            x = _layer(x, l, layers[l])
        return x

    return (trunk(x_prompt), trunk(x_sample))
```

```python
import functools
import math

import jax
import jax.numpy as jnp
from jax import lax
from jax.experimental import pallas as pl
from jax.experimental.pallas import tpu as pltpu

F32 = jnp.float32
BF16 = jnp.bfloat16

D_MODEL = 2048
D_FF = 5632
DA_HEADS = 8
DA_DK = 64
DA_DV = 128
SW_Q_HEADS = 16
SW_KV_HEADS = 4
SW_DH = 64
WINDOW = 128
N_BUCKETS = 32
MAX_DISTANCE = 128
EPS = 1e-6
D_IN = 4608
LANES = 128
N_SLABS = D_IN // LANES
QK_SCALE = 0.125
NEG_BIG = -1e30

PROJ_TM = 512
PROJ_TN = 512
DA_TQ = 256
DA_TILES = 4
DA_CK = 512
SW_TQ = 256
VMEM_LIMIT = 56 * 1024 * 1024


def _rms(x, g):
    return x * lax.rsqrt(jnp.mean(x * x, axis=-1, keepdims=True) + EPS) * g


def _ffn_kernel(x_ref, gpre_ref, wg_ref, wu_ref, wd_ref, gpost_ref, o_ref, xn_ref, *, row_split):
    j = pl.program_id(1)

    @pl.when(j == 0)
    def _():
        xn_ref[...] = _rms(x_ref[...], gpre_ref[...]).astype(BF16)
        o_ref[...] = jnp.zeros_like(o_ref)

    rows = xn_ref.shape[0] // row_split
    for r in range(row_split):
        rs = slice(r * rows, (r + 1) * rows)
        xn = xn_ref[rs, :]
        g = jnp.dot(xn, wg_ref[...], preferred_element_type=F32)
        u = jnp.dot(xn, wu_ref[...], preferred_element_type=F32)
        a = (g / (1.0 + jnp.exp(-g)) * u).astype(BF16)
        o_ref[rs, :] += jnp.dot(a, wd_ref[...], preferred_element_type=F32)

    @pl.when(j == pl.num_programs(1) - 1)
    def _():
        o_ref[...] = x_ref[...] + 0.5 * _rms(o_ref[...], gpost_ref[...])


def _ffn(x, g_pre, w_gu, w_down, g_post, *, tm, tf, row_split=1, w_bufs=2, vmem=VMEM_LIMIT):
    T = x.shape[0]
    nf = D_FF // tf
    wmode = {} if w_bufs == 2 else {"pipeline_mode": pl.Buffered(w_bufs)}
    return pl.pallas_call(
        functools.partial(_ffn_kernel, row_split=row_split),
        grid=(T // tm, nf),
        in_specs=[
            pl.BlockSpec((tm, D_MODEL), lambda i, j: (i, 0)),
            pl.BlockSpec((1, D_MODEL), lambda i, j: (0, 0)),
            pl.BlockSpec((D_MODEL, tf), lambda i, j: (0, j), **wmode),
            pl.BlockSpec((D_MODEL, tf), lambda i, j: (0, j + nf), **wmode),
            pl.BlockSpec((tf, D_MODEL), lambda i, j: (j, 0), **wmode),
            pl.BlockSpec((1, D_MODEL), lambda i, j: (0, 0)),
        ],
        out_specs=pl.BlockSpec((tm, D_MODEL), lambda i, j: (i, 0)),
        out_shape=jax.ShapeDtypeStruct((T, D_MODEL), F32),
        scratch_shapes=[pltpu.VMEM((tm, D_MODEL), BF16)],
        compiler_params=pltpu.CompilerParams(
            dimension_semantics=("parallel", "arbitrary"),
            vmem_limit_bytes=vmem),
        name="ffn",
    )(x, g_pre, w_gu, w_gu, w_down, g_post)


def _proj_kernel(h_ref, g_ref, w_ref, o_ref):
    n = _rms(h_ref[...], g_ref[...]).astype(BF16)
    per = PROJ_TN // LANES
    for c in range(D_IN // PROJ_TN):
        r = jnp.dot(n, w_ref[:, c * PROJ_TN:(c + 1) * PROJ_TN], preferred_element_type=F32)
        for s in range(per):
            slab = c * per + s
            blk = r[:, s * LANES:(s + 1) * LANES]
            if slab < 8 or 24 <= slab < 32:
                blk = blk * QK_SCALE
            o_ref[slab] = blk.astype(BF16)


def _in_proj(h, g, w_in):
    T = h.shape[0]
    return pl.pallas_call(
        _proj_kernel,
        grid=(T // PROJ_TM,),
        in_specs=[
            pl.BlockSpec((PROJ_TM, D_MODEL), lambda i: (i, 0)),
            pl.BlockSpec((1, D_MODEL), lambda i: (0, 0)),
            pl.BlockSpec((D_MODEL, D_IN), lambda i: (0, 0), pipeline_mode=pl.Buffered(1)),
        ],
        out_specs=pl.BlockSpec((N_SLABS, PROJ_TM, LANES), lambda i: (0, i, 0)),
        out_shape=jax.ShapeDtypeStruct((N_SLABS, T, LANES), BF16),
        compiler_params=pltpu.CompilerParams(
            dimension_semantics=("parallel",),
            vmem_limit_bytes=VMEM_LIMIT),
        name="in_proj",
    )(h, g, w_in)


def _t5_bucket(rp):
    half = N_BUCKETS // 2
    max_exact = half // 2
    ret = jnp.where(rp > 0, half, 0)
    n = jnp.abs(rp)
    nf = jnp.maximum(n, 1).astype(jnp.float32)
    large = max_exact + (jnp.log(nf / max_exact) / math.log(MAX_DISTANCE / max_exact)
                         * (half - max_exact)).astype(jnp.int32)
    large = jnp.minimum(large, half - 1)
    return ret + jnp.where(n < max_exact, n, large)


def _bias_lookup(table, rp):
    bucket = _t5_bucket(rp)[None]
    col = lambda b: table[b].astype(F32).reshape((-1,) + (1,) * rp.ndim)
    out = jnp.broadcast_to(col(0), (table.shape[1],) + rp.shape)
    for b in range(1, N_BUCKETS):
        out = jnp.where(bucket == b, col(b), out)
    return out


def _da_bias_shift(ck):
    return ck + WINDOW


def _da_bias_table(rel_bias, tq, ck):
    nblk = (2 * ck + tq + 2 * WINDOW) // LANES
    rp = (LANES * jnp.arange(nblk)[:, None, None] + jnp.arange(LANES)[None, None, :]
          - jnp.arange(tq)[None, :, None] - _da_bias_shift(ck))
    return _bias_lookup(rel_bias[:, :DA_HEADS], rp)


def _sw_bias_table(rel_bias, tq):
    wk = tq + 2 * WINDOW
    starts = jnp.array([0, -WINDOW, -2 * WINDOW])
    rp = starts[:, None, None] + jnp.arange(wk)[None, None, :] - jnp.arange(tq)[None, :, None]
    tab = _bias_lookup(rel_bias[:, DA_HEADS:], rp)
    return jnp.where((jnp.abs(rp) <= WINDOW)[None], tab, NEG_BIG)


def _da_kernel(lq1_ref, lk1_ref, lq2_ref, lk2_ref, q_ref, k_ref, v_ref, bias_ref, g_ref,
               o_ref, s_ref, vext_ref, *, s_len, lambda_init):
    tq, ck, ntile = DA_TQ, DA_CK, DA_TILES
    nck = s_len // ck
    per = ck // LANES
    t = pl.program_id(2)

    @pl.when(t == 0)
    def _():
        vext_ref[:, :LANES] = v_ref[0]
        vext_ref[:, LANES:] = jnp.ones((s_len, LANES), BF16)

    lo = lax.broadcasted_iota(jnp.int32, (tq, LANES), 1) < DA_DK
    dmin = -_da_bias_shift(ck)
    dmax = tq + WINDOW
    lam = (jnp.exp(jnp.sum(lq1_ref[...] * lk1_ref[...], axis=1, keepdims=True))
           - jnp.exp(jnp.sum(lq2_ref[...] * lk2_ref[...], axis=1, keepdims=True))
           + lambda_init)

    def scores(i):
        q0 = (t * ntile + i) * tq
        q = q_ref[0, i * tq:(i + 1) * tq, :]
        zero = jnp.zeros_like(q)
        qs = jnp.concatenate([jnp.where(lo, q, zero), jnp.where(lo, zero, q)], axis=0)
        mx1 = jnp.full((tq, LANES), -jnp.inf, F32)
        mx2 = mx1
        for c in range(nck):
            kc = k_ref[0, c * ck:(c + 1) * ck, :]
            s = lax.dot_general(qs, kc, (((1,), (1,)), ((), ())), preferred_element_type=F32)
            ob = (jnp.clip(c * ck - q0, dmin, dmax) - dmin) // LANES
            for jb in range(per):
                b = bias_ref[0, ob + jb]
                cols = slice(jb * LANES, (jb + 1) * LANES)
                s1 = s[:tq, cols] + b
                s2 = s[tq:, cols] + b
                dst = slice(c * ck + jb * LANES, c * ck + (jb + 1) * LANES)
                s_ref[i, :tq, dst] = s1
                s_ref[i, tq:, dst] = s2
                mx1 = jnp.maximum(mx1, s1)
                mx2 = jnp.maximum(mx2, s2)
        mx = jnp.concatenate([mx1, mx2], axis=0)
        return jnp.broadcast_to(jnp.max(mx, axis=1, keepdims=True), (2 * tq, LANES))

    def attend(i, m):
        acc = jnp.zeros((2 * tq, 2 * LANES), F32)
        for c in range(nck):
            p = jnp.concatenate(
                [jnp.exp(s_ref[i, :, c * ck + jb * LANES:c * ck + (jb + 1) * LANES] - m).astype(BF16)
                 for jb in range(per)], axis=1)
            acc = acc + jnp.dot(p, vext_ref[c * ck:(c + 1) * ck, :], preferred_element_type=F32)
        o = acc[:, :LANES] / acc[:, LANES:]
        o = o[:tq] - lam * o[tq:]
        o_ref[0, i * tq:(i + 1) * tq, :] = (_rms(o, g_ref[...]) * (1.0 - lambda_init)).astype(BF16)

    ms = [scores(i) for i in range(ntile)]
    for i in range(ntile):
        attend(i, ms[i])


def _diff_attn(proj, bias_tab, lq1, lk1, lq2, lk2, g_subln, batch, s_len, lambda_init):
    T = batch * s_len
    rows = DA_TQ * DA_TILES
    nt = s_len // rows
    nblk = bias_tab.shape[1]
    vec = pl.BlockSpec((1, DA_DK), lambda h, b, t: (0, 0))
    return pl.pallas_call(
        functools.partial(_da_kernel, s_len=s_len, lambda_init=lambda_init),
        grid=(DA_HEADS, batch, nt),
        in_specs=[
            vec, vec, vec, vec,
            pl.BlockSpec((1, rows, LANES), lambda h, b, t: (h, b * nt + t, 0)),
            pl.BlockSpec((1, s_len, LANES), lambda h, b, t: (8 + h, b, 0)),
            pl.BlockSpec((1, s_len, LANES), lambda h, b, t: (16 + h, b, 0)),
            pl.BlockSpec((1, nblk, DA_TQ, LANES), lambda h, b, t: (h, 0, 0, 0)),
            pl.BlockSpec((1, DA_DV), lambda h, b, t: (0, 0)),
        ],
        out_specs=pl.BlockSpec((1, rows, LANES), lambda h, b, t: (h, b * nt + t, 0)),
        out_shape=jax.ShapeDtypeStruct((DA_HEADS, T, LANES), BF16),
        scratch_shapes=[pltpu.VMEM((DA_TILES, 2 * DA_TQ, s_len), F32),
                        pltpu.VMEM((s_len, 2 * LANES), BF16)],
        compiler_params=pltpu.CompilerParams(
            dimension_semantics=("parallel", "parallel", "arbitrary"),
            vmem_limit_bytes=VMEM_LIMIT),
        name="diff_attn",
    )(lq1, lk1, lq2, lk2, proj, proj, proj, bias_tab, g_subln)


def _swap_halves(x):
    return jnp.concatenate([x[:, SW_DH:], x[:, :SW_DH]], axis=1)


def _sw_kernel(sink_ref, q_ref, k_ref, v_ref, bias_ref, o_ref, vext_ref, *, s_len):
    tq = SW_TQ
    wk = tq + 2 * WINDOW
    per = wk // LANES
    p = pl.program_id(0)
    t = pl.program_id(2)

    @pl.when(t == 0)
    def _():
        vext_ref[:, :LANES] = v_ref[0]
        vext_ref[:, LANES:] = jnp.ones((s_len, LANES), BF16)

    start = pl.multiple_of(jnp.clip(t * tq - WINDOW, 0, s_len - wk), LANES)
    kw = k_ref[0, pl.ds(start, wk), :]
    vw = vext_ref[pl.ds(start, wk), :]
    lo = lax.broadcasted_iota(jnp.int32, (tq, LANES), 1) < SW_DH

    outs = []
    for j in range(8):
        kv_half, q_half = j // 4, j % 2
        xq = q_ref[j // 2]
        if q_half != kv_half:
            xq = _swap_halves(xq)
        zero = jnp.zeros_like(xq)
        xq = jnp.where(lo, xq, zero) if kv_half == 0 else jnp.where(lo, zero, xq)
        s = lax.dot_general(xq, kw, (((1,), (1,)), ((), ())), preferred_element_type=F32)
        sb = [s[:, jb * LANES:(jb + 1) * LANES] + bias_ref[j, 0, :, jb * LANES:(jb + 1) * LANES]
              for jb in range(per)]
        mx = sb[0]
        for blk in sb[1:]:
            mx = jnp.maximum(mx, blk)
        sk = sink_ref[p * 8 + j]
        m = jnp.broadcast_to(jnp.maximum(jnp.max(mx, axis=1, keepdims=True), sk), (tq, LANES))
        e = jnp.concatenate([jnp.exp(blk - m).astype(BF16) for blk in sb], axis=1)
        r = jnp.dot(e, vw, preferred_element_type=F32)
        o = r[:, :LANES] / (r[:, LANES:] + jnp.exp(sk - m))
        if q_half != kv_half:
            o = _swap_halves(o)
        outs.append(o)
    for ob in range(4):
        o_ref[ob] = jnp.where(lo, outs[2 * ob], outs[2 * ob + 1]).astype(BF16)


def _win_attn(proj, bias_tab, sink, batch, s_len):
    T = batch * s_len
    nt = s_len // SW_TQ
    wk = SW_TQ + 2 * WINDOW

    def case(t):
        return jnp.where(t == 0, 0, jnp.where(t == nt - 1, 2, 1))

    return pl.pallas_call(
        functools.partial(_sw_kernel, s_len=s_len),
        grid=(2, batch, nt),
        in_specs=[
            pl.BlockSpec(memory_space=pltpu.SMEM),
            pl.BlockSpec((4, SW_TQ, LANES), lambda p, b, t: (6 + p, b * nt + t, 0)),
            pl.BlockSpec((1, s_len, LANES), lambda p, b, t: (32 + p, b, 0)),
            pl.BlockSpec((1, s_len, LANES), lambda p, b, t: (34 + p, b, 0)),
            pl.BlockSpec((8, 1, SW_TQ, wk), lambda p, b, t: (p, case(t), 0, 0)),
        ],
        out_specs=pl.BlockSpec((4, SW_TQ, LANES), lambda p, b, t: (p, b * nt + t, 0)),
        out_shape=jax.ShapeDtypeStruct((SW_Q_HEADS // 2, T, LANES), BF16),
        scratch_shapes=[pltpu.VMEM((s_len, 2 * LANES), BF16)],
        compiler_params=pltpu.CompilerParams(
            dimension_semantics=("parallel", "parallel", "arbitrary"),
            vmem_limit_bytes=VMEM_LIMIT),
        name="win_attn",
    )(sink, proj, proj, proj, bias_tab)


def _out_kernel(da_ref, sw_ref, h_ref, w_ref, g_ref, o_ref):
    a = jnp.concatenate([da_ref[c] for c in range(8)] + [sw_ref[c] for c in range(8)], axis=1)
    mix = jnp.dot(a, w_ref[...], preferred_element_type=F32)
    o_ref[...] = h_ref[...] + _rms(mix, g_ref[...])


def _out_proj(o_da, o_sw, h, w_out, g):
    T = h.shape[0]
    tm = PROJ_TM
    return pl.pallas_call(
        _out_kernel,
        grid=(T // tm,),
        in_specs=[
            pl.BlockSpec((8, tm, LANES), lambda i: (0, i, 0)),
            pl.BlockSpec((8, tm, LANES), lambda i: (0, i, 0)),
            pl.BlockSpec((tm, D_MODEL), lambda i: (i, 0)),
            pl.BlockSpec((D_MODEL, D_MODEL), lambda i: (0, 0), pipeline_mode=pl.Buffered(1)),
            pl.BlockSpec((1, D_MODEL), lambda i: (0, 0)),
        ],
        out_specs=pl.BlockSpec((tm, D_MODEL), lambda i: (i, 0)),
        out_shape=jax.ShapeDtypeStruct((T, D_MODEL), F32),
        compiler_params=pltpu.CompilerParams(
            dimension_semantics=("parallel",),
            vmem_limit_bytes=VMEM_LIMIT),
        name="out_proj",
    )(o_da, o_sw, h, w_out, g)


def _layer(x, l, wts, ffn1_cfg, ffn2_cfg):
    batch, s_len, _ = x.shape
    assert s_len % DA_TQ == 0 and s_len % DA_CK == 0 and s_len % SW_TQ == 0
    assert s_len >= SW_TQ + 2 * WINDOW
    lambda_init = 0.8 - 0.6 * math.exp(-0.3 * l)
    x2 = x.reshape(batch * s_len, D_MODEL)
    h = _ffn(x2, wts["g_ffn1_pre"], wts["w_ffn1_gu"], wts["w_ffn1_down"], wts["g_ffn1_post"], **ffn1_cfg)
    proj = _in_proj(h, wts["g_mix_pre"], wts["w_in"])
    o_da = _diff_attn(proj, wts["da_bias"], wts["lambda_q1"], wts["lambda_k1"], wts["lambda_q2"],
                      wts["lambda_k2"], wts["g_diff_subln"], batch, s_len, lambda_init)
    o_sw = _win_attn(proj, wts["sw_bias"], wts["sink_logit"], batch, s_len)
    h2 = _out_proj(o_da, o_sw, h, wts["w_out"], wts["g_mix_post"])
    y = _ffn(h2, wts["g_ffn2_pre"], wts["w_ffn2_gu"], wts["w_ffn2_down"], wts["g_ffn2_post"], **ffn2_cfg)
    return y.reshape(batch, s_len, D_MODEL)


def kernel(x_prompt, x_sample, rel_bias, g_ffn1_pre, w_ffn1_gu, w_ffn1_down, g_ffn1_post, g_mix_pre, w_in, lambda_q1, lambda_k1, lambda_q2, lambda_k2, g_diff_subln, sink_logit, w_out, g_mix_post, g_ffn2_pre, w_ffn2_gu, w_ffn2_down, g_ffn2_post):
    depth = w_in.shape[0]
    layers = []
    for l in range(depth):
        layers.append({
            "g_ffn1_pre": g_ffn1_pre[l][None], "g_ffn1_post": g_ffn1_post[l][None],
            "w_ffn1_gu": w_ffn1_gu[l].astype(BF16), "w_ffn1_down": w_ffn1_down[l].astype(BF16),
            "g_mix_pre": g_mix_pre[l][None], "w_in": w_in[l].astype(BF16),
            "lambda_q1": lambda_q1[l][None], "lambda_k1": lambda_k1[l][None],
            "lambda_q2": lambda_q2[l][None], "lambda_k2": lambda_k2[l][None],
            "g_diff_subln": g_diff_subln[l][None], "sink_logit": sink_logit[l],
            "w_out": w_out[l].astype(BF16), "g_mix_post": g_mix_post[l][None],
            "g_ffn2_pre": g_ffn2_pre[l][None], "g_ffn2_post": g_ffn2_post[l][None],
            "w_ffn2_gu": w_ffn2_gu[l].astype(BF16), "w_ffn2_down": w_ffn2_down[l].astype(BF16),
            "da_bias": _da_bias_table(rel_bias, DA_TQ, DA_CK),
            "sw_bias": _sw_bias_table(rel_bias, SW_TQ),
        })

    def trunk(x, ffn1_cfg, ffn2_cfg):
        for l in range(depth):
            x = _layer(x, l, layers[l], ffn1_cfg, ffn2_cfg)
        return x

    cfg_a = dict(tm=512, tf=512)
    cfg_b = dict(tm=1024, tf=256, row_split=2)
    cfg_c = dict(tm=1024, tf=512, row_split=2, vmem=62 * 1024 * 1024)
    cfg_d = dict(tm=1024, tf=256, row_split=1)
    return (trunk(x_prompt, cfg_a, cfg_b), trunk(x_sample, cfg_c, cfg_d))
```

```python
import functools
import math

import jax
import jax.numpy as jnp
from jax import lax
from jax.experimental import pallas as pl
from jax.experimental.pallas import tpu as pltpu

F32 = jnp.float32
BF16 = jnp.bfloat16

D_MODEL = 2048
D_FF = 5632
DA_HEADS = 8
DA_DK = 64
DA_DV = 128
SW_Q_HEADS = 16
SW_KV_HEADS = 4
SW_DH = 64
WINDOW = 128
N_BUCKETS = 32
MAX_DISTANCE = 128
EPS = 1e-6
D_IN = 4608
LANES = 128
N_SLABS = D_IN // LANES
LOG2E = math.log2(math.e)
Q_SCALE = 0.125 * LOG2E
NEG_BIG = -1e30

FFN_TM = 1024
FFN_TF = 512
FFN_ROW_SPLIT = 2
NORM_ROWS = 16
PROJ_TM = 512
PROJ_TN = 512
DA_TQ = 256
DA_TILES = 4
DA_PV_GROUP = 4
SW_TQ = 256
VMEM_LIMIT = 56 * 1024 * 1024
FFN_VMEM_LIMIT = 62 * 1024 * 1024


def _rms(x, g):
    return x * lax.rsqrt(jnp.mean(x * x, axis=-1, keepdims=True) + EPS) * g


def _ffn_kernel(x_ref, gpre_ref, wg_ref, wu_ref, wd_ref, ghalf_ref, o_ref, xn_ref):
    j = pl.program_id(1)
    tm = xn_ref.shape[0]

    @pl.when(j == 0)
    def _():
        for i in range(tm // NORM_ROWS):
            rs = slice(i * NORM_ROWS, (i + 1) * NORM_ROWS)
            xn_ref[rs, :] = _rms(x_ref[rs, :], gpre_ref[...]).astype(BF16)
            o_ref[rs, :] = jnp.zeros((NORM_ROWS, D_MODEL), F32)

    rows = tm // FFN_ROW_SPLIT
    for r in range(FFN_ROW_SPLIT):
        rs = slice(r * rows, (r + 1) * rows)
        xn = xn_ref[rs, :]
        g = jnp.dot(xn, wg_ref[...], preferred_element_type=F32)
        u = jnp.dot(xn, wu_ref[...], preferred_element_type=F32)
        a = (g / (1.0 + jnp.exp(-g)) * u).astype(BF16)
        o_ref[rs, :] += jnp.dot(a, wd_ref[...], preferred_element_type=F32)

    @pl.when(j == pl.num_programs(1) - 1)
    def _():
        for i in range(tm // NORM_ROWS):
            rs = slice(i * NORM_ROWS, (i + 1) * NORM_ROWS)
            o_ref[rs, :] = x_ref[rs, :] + _rms(o_ref[rs, :], ghalf_ref[...])


def _ffn(x, g_pre, w_gu, w_down, g_post_half):
    T = x.shape[0]
    tm, tf = FFN_TM, FFN_TF
    nf = D_FF // tf
    return pl.pallas_call(
        _ffn_kernel,
        grid=(T // tm, nf),
        in_specs=[
            pl.BlockSpec((tm, D_MODEL), lambda i, j: (i, 0)),
            pl.BlockSpec((1, D_MODEL), lambda i, j: (0, 0)),
            pl.BlockSpec((D_MODEL, tf), lambda i, j: (0, j)),
            pl.BlockSpec((D_MODEL, tf), lambda i, j: (0, j + nf)),
            pl.BlockSpec((tf, D_MODEL), lambda i, j: (j, 0)),
            pl.BlockSpec((1, D_MODEL), lambda i, j: (0, 0)),
        ],
        out_specs=pl.BlockSpec((tm, D_MODEL), lambda i, j: (i, 0)),
        out_shape=jax.ShapeDtypeStruct((T, D_MODEL), F32),
        scratch_shapes=[pltpu.VMEM((tm, D_MODEL), BF16)],
        compiler_params=pltpu.CompilerParams(
            dimension_semantics=("parallel", "arbitrary"),
            vmem_limit_bytes=FFN_VMEM_LIMIT),
        name="ffn",
    )(x, g_pre, w_gu, w_gu, w_down, g_post_half)


def _proj_kernel(h_ref, g_ref, w_ref, o_ref):
    n = _rms(h_ref[...], g_ref[...]).astype(BF16)
    per = PROJ_TN // LANES
    for c in range(D_IN // PROJ_TN):
        r = jnp.dot(n, w_ref[:, c * PROJ_TN:(c + 1) * PROJ_TN], preferred_element_type=F32)
        for s in range(per):
            slab = c * per + s
            blk = r[:, s * LANES:(s + 1) * LANES]
            if slab < 8 or 24 <= slab < 32:
                blk = blk * Q_SCALE
            o_ref[slab] = blk.astype(BF16)


def _in_proj(h, g, w_in):
    T = h.shape[0]
    return pl.pallas_call(
        _proj_kernel,
        grid=(T // PROJ_TM,),
        in_specs=[
            pl.BlockSpec((PROJ_TM, D_MODEL), lambda i: (i, 0)),
            pl.BlockSpec((1, D_MODEL), lambda i: (0, 0)),
            pl.BlockSpec((D_MODEL, D_IN), lambda i: (0, 0), pipeline_mode=pl.Buffered(1)),
        ],
        out_specs=pl.BlockSpec((N_SLABS, PROJ_TM, LANES), lambda i: (0, i, 0)),
        out_shape=jax.ShapeDtypeStruct((N_SLABS, T, LANES), BF16),
        compiler_params=pltpu.CompilerParams(
            dimension_semantics=("parallel",),
            vmem_limit_bytes=VMEM_LIMIT),
        name="in_proj",
    )(h, g, w_in)


def _t5_bucket(rp):
    half = N_BUCKETS // 2
    max_exact = half // 2
    ret = jnp.where(rp > 0, half, 0)
    n = jnp.abs(rp)
    nf = jnp.maximum(n, 1).astype(jnp.float32)
    large = max_exact + (jnp.log(nf / max_exact) / math.log(MAX_DISTANCE / max_exact)
                         * (half - max_exact)).astype(jnp.int32)
    large = jnp.minimum(large, half - 1)
    return ret + jnp.where(n < max_exact, n, large)


def _bias_lookup(table, rp):
    bucket = _t5_bucket(rp)[None]
    col = lambda b: (table[b].astype(F32) * LOG2E).reshape((-1,) + (1,) * rp.ndim)
    out = jnp.broadcast_to(col(0), (table.shape[1],) + rp.shape)
    for b in range(1, N_BUCKETS):
        out = jnp.where(bucket == b, col(b), out)
    return out


DA_BAND_OFFSETS = (-WINDOW, 0, WINDOW, 2 * WINDOW)
DA_IDX_LEFT = len(DA_BAND_OFFSETS)
DA_IDX_RIGHT = DA_IDX_LEFT + 1
FAR = 2 * (DA_TQ + 2 * WINDOW)


def _da_bias_table(rel_bias):
    offs = jnp.array(DA_BAND_OFFSETS + (-FAR, FAR))
    rp = offs[:, None, None] + jnp.arange(LANES)[None, None, :] - jnp.arange(DA_TQ)[None, :, None]
    return _bias_lookup(rel_bias[:, :DA_HEADS], rp)


def _da_far_bias(rel_bias):
    return _bias_lookup(rel_bias[:, :DA_HEADS], jnp.array([FAR, -FAR]))


def _sw_bias_table(rel_bias, tq):
    wk = tq + 2 * WINDOW
    starts = jnp.array([0, -WINDOW, -2 * WINDOW])
    rp = starts[:, None, None] + jnp.arange(wk)[None, None, :] - jnp.arange(tq)[None, :, None]
    tab = _bias_lookup(rel_bias[:, DA_HEADS:], rp)
    return jnp.where((jnp.abs(rp) <= WINDOW)[None], tab, NEG_BIG)


def _da_kernel(far_ref, lq1_ref, lk1_ref, lq2_ref, lk2_ref, q_ref, k_ref, v_ref, bias_ref, g_ref,
               o_ref, s_ref, off_ref, vext_ref, *, s_len, lambda_init):
    tq, ntile = DA_TQ, DA_TILES
    ck = tq
    nck = s_len // ck
    h = pl.program_id(0)
    t = pl.program_id(2)

    @pl.when(t == 0)
    def _():
        vext_ref[:, :LANES] = v_ref[0]
        vext_ref[:, LANES:] = jnp.ones((s_len, LANES), BF16)

    c_right = far_ref[h, 0]
    c_left = far_ref[h, 1]
    lo = lax.broadcasted_iota(jnp.int32, (tq, LANES), 1) < DA_DK
    lam = (jnp.exp(jnp.sum(lq1_ref[...] * lk1_ref[...], axis=1, keepdims=True))
           - jnp.exp(jnp.sum(lq2_ref[...] * lk2_ref[...], axis=1, keepdims=True))
           + lambda_init)

    def chunks(i):
        q0 = (t * ntile + i) * tq
        out = []
        for r in range(nck):
            u = q0 + (r - 1) * ck
            wrapped = (u < 0) if r == 0 else (u >= s_len)
            k0 = jnp.where(wrapped, u + s_len if r == 0 else u - s_len, u)
            out.append((pl.multiple_of(k0, ck), wrapped))
        return out

    def scores(i, ch):
        q = q_ref[0, i * tq:(i + 1) * tq, :]
        zero = jnp.zeros_like(q)
        qs = jnp.concatenate([jnp.where(lo, q, zero), jnp.where(lo, zero, q)], axis=0)
        w0, w2 = ch[0][1], ch[2][1]
        near = [
            (jnp.where(w0, DA_IDX_RIGHT, DA_IDX_LEFT), jnp.where(w0, DA_IDX_RIGHT, 0)),
            (1, 2),
            (jnp.where(w2, DA_IDX_LEFT, 3), jnp.where(w2, DA_IDX_LEFT, DA_IDX_RIGHT)),
        ]
        mx = jnp.full((2 * tq, LANES), -jnp.inf, F32)
        for r in range(nck):
            kc = k_ref[0, pl.ds(ch[r][0], ck), :]
            s = lax.dot_general(qs, kc, (((1,), (1,)), ((), ())), preferred_element_type=F32)
            if r < 3:
                for jb in range(2):
                    tile = bias_ref[0, near[r][jb]]
                    tile = jnp.concatenate([tile, tile], axis=0)
                    sb = s[:, jb * LANES:(jb + 1) * LANES] + tile
                    s_ref[i, :, r * ck + jb * LANES:r * ck + (jb + 1) * LANES] = sb
                    mx = jnp.maximum(mx, sb)
            else:
                s_ref[i, :, r * ck:(r + 1) * ck] = s
                c_far = jnp.where(ch[r][1], c_left, c_right)
                mx = jnp.maximum(mx, jnp.maximum(s[:, :LANES], s[:, LANES:]) + c_far)
        m = jnp.broadcast_to(jnp.max(mx, axis=1, keepdims=True), (2 * tq, LANES))
        off_ref[i, 0] = m - c_right
        off_ref[i, 1] = m - c_left
        return m

    def attend(i, ch, m):
        acc = jnp.zeros((2 * tq, 2 * LANES), F32)
        for g0 in range(0, nck, DA_PV_GROUP):
            ps, vs = [], []
            for r in range(g0, g0 + DA_PV_GROUP):
                off = m if r < 3 else off_ref[i, ch[r][1].astype(jnp.int32)]
                for jb in range(2):
                    blk = s_ref[i, :, r * ck + jb * LANES:r * ck + (jb + 1) * LANES]
                    ps.append(jnp.exp2(blk - off).astype(BF16))
                vs.append(vext_ref[pl.ds(ch[r][0], ck), :])
            acc = acc + jnp.dot(jnp.concatenate(ps, axis=1), jnp.concatenate(vs, axis=0),
                                preferred_element_type=F32)
        o = acc[:, :LANES] / acc[:, LANES:]
        o = o[:tq] - lam * o[tq:]
        o_ref[0, i * tq:(i + 1) * tq, :] = (_rms(o, g_ref[...]) * (1.0 - lambda_init)).astype(BF16)

    chs = [chunks(i) for i in range(ntile)]
    ms = [scores(i, chs[i]) for i in range(ntile)]
    for i in range(ntile):
        attend(i, chs[i], ms[i])


def _diff_attn(proj, bias_tab, far_bias, lq1, lk1, lq2, lk2, g_subln, batch, s_len, lambda_init):
    T = batch * s_len
    rows = DA_TQ * DA_TILES
    nt = s_len // rows
    ntab = bias_tab.shape[1]
    vec = pl.BlockSpec((1, DA_DK), lambda h, b, t: (0, 0))
    return pl.pallas_call(
        functools.partial(_da_kernel, s_len=s_len, lambda_init=lambda_init),
        grid=(DA_HEADS, batch, nt),
        in_specs=[
            pl.BlockSpec(memory_space=pltpu.SMEM),
            vec, vec, vec, vec,
            pl.BlockSpec((1, rows, LANES), lambda h, b, t: (h, b * nt + t, 0)),
            pl.BlockSpec((1, s_len, LANES), lambda h, b, t: (8 + h, b, 0)),
            pl.BlockSpec((1, s_len, LANES), lambda h, b, t: (16 + h, b, 0)),
            pl.BlockSpec((1, ntab, DA_TQ, LANES), lambda h, b, t: (h, 0, 0, 0)),
            pl.BlockSpec((1, DA_DV), lambda h, b, t: (0, 0)),
        ],
        out_specs=pl.BlockSpec((1, rows, LANES), lambda h, b, t: (h, b * nt + t, 0)),
        out_shape=jax.ShapeDtypeStruct((DA_HEADS, T, LANES), BF16),
        scratch_shapes=[pltpu.VMEM((DA_TILES, 2 * DA_TQ, s_len), F32),
                        pltpu.VMEM((DA_TILES, 2, 2 * DA_TQ, LANES), F32),
                        pltpu.VMEM((s_len, 2 * LANES), BF16)],
        compiler_params=pltpu.CompilerParams(
            dimension_semantics=("parallel", "parallel", "arbitrary"),
            vmem_limit_bytes=VMEM_LIMIT),
        name="diff_attn",
    )(far_bias, lq1, lk1, lq2, lk2, proj, proj, proj, bias_tab, g_subln)


def _swap_halves(x):
    return jnp.concatenate([x[:, SW_DH:], x[:, :SW_DH]], axis=1)


def _sw_kernel(sink_ref, q_ref, k_ref, v_ref, bias_ref, o_ref, vext_ref, *, s_len):
    tq = SW_TQ
    wk = tq + 2 * WINDOW
    per = wk // LANES
    p = pl.program_id(0)
    t = pl.program_id(2)

    @pl.when(t == 0)
    def _():
        vext_ref[:, :LANES] = v_ref[0]
        vext_ref[:, LANES:] = jnp.ones((s_len, LANES), BF16)

    start = pl.multiple_of(jnp.clip(t * tq - WINDOW, 0, s_len - wk), LANES)
    kw = k_ref[0, pl.ds(start, wk), :]
    vw = vext_ref[pl.ds(start, wk), :]
    lo = lax.broadcasted_iota(jnp.int32, (tq, LANES), 1) < SW_DH

    outs = []
    for j in range(8):
        kv_half, q_half = j // 4, j % 2
        xq = q_ref[j // 2]
        if q_half != kv_half:
            xq = _swap_halves(xq)
        zero = jnp.zeros_like(xq)
        xq = jnp.where(lo, xq, zero) if kv_half == 0 else jnp.where(lo, zero, xq)
        s = lax.dot_general(xq, kw, (((1,), (1,)), ((), ())), preferred_element_type=F32)
        sb = [s[:, jb * LANES:(jb + 1) * LANES] + bias_ref[j, 0, :, jb * LANES:(jb + 1) * LANES]
              for jb in range(per)]
        mx = sb[0]
        for blk in sb[1:]:
            mx = jnp.maximum(mx, blk)
        sk = sink_ref[p * 8 + j]
        m = jnp.broadcast_to(jnp.maximum(jnp.max(mx, axis=1, keepdims=True), sk), (tq, LANES))
        e = jnp.concatenate([jnp.exp2(blk - m).astype(BF16) for blk in sb], axis=1)
        r = jnp.dot(e, vw, preferred_element_type=F32)
        o = r[:, :LANES] / (r[:, LANES:] + jnp.exp2(sk - m))
        if q_half != kv_half:
            o = _swap_halves(o)
        outs.append(o)
    for ob in range(4):
        o_ref[ob] = jnp.where(lo, outs[2 * ob], outs[2 * ob + 1]).astype(BF16)


def _win_attn(proj, bias_tab, sink, batch, s_len):
    T = batch * s_len
    nt = s_len // SW_TQ
    wk = SW_TQ + 2 * WINDOW

    def case(t):
        return jnp.where(t == 0, 0, jnp.where(t == nt - 1, 2, 1))

    return pl.pallas_call(
        functools.partial(_sw_kernel, s_len=s_len),
        grid=(2, batch, nt),
        in_specs=[
            pl.BlockSpec(memory_space=pltpu.SMEM),
            pl.BlockSpec((4, SW_TQ, LANES), lambda p, b, t: (6 + p, b * nt + t, 0)),
            pl.BlockSpec((1, s_len, LANES), lambda p, b, t: (32 + p, b, 0)),
            pl.BlockSpec((1, s_len, LANES), lambda p, b, t: (34 + p, b, 0)),
            pl.BlockSpec((8, 1, SW_TQ, wk), lambda p, b, t: (p, case(t), 0, 0)),
        ],
        out_specs=pl.BlockSpec((4, SW_TQ, LANES), lambda p, b, t: (p, b * nt + t, 0)),
        out_shape=jax.ShapeDtypeStruct((SW_Q_HEADS // 2, T, LANES), BF16),
        scratch_shapes=[pltpu.VMEM((s_len, 2 * LANES), BF16)],
        compiler_params=pltpu.CompilerParams(
            dimension_semantics=("parallel", "parallel", "arbitrary"),
            vmem_limit_bytes=VMEM_LIMIT),
        name="win_attn",
    )(sink, proj, proj, proj, bias_tab)


def _out_kernel(da_ref, sw_ref, h_ref, w_ref, g_ref, o_ref):
    a = jnp.concatenate([da_ref[c] for c in range(8)] + [sw_ref[c] for c in range(8)], axis=1)
    mix = jnp.dot(a, w_ref[...], preferred_element_type=F32)
    o_ref[...] = h_ref[...] + _rms(mix, g_ref[...])


def _out_proj(o_da, o_sw, h, w_out, g):
    T = h.shape[0]
    tm = PROJ_TM
    return pl.pallas_call(
        _out_kernel,
        grid=(T // tm,),
        in_specs=[
            pl.BlockSpec((8, tm, LANES), lambda i: (0, i, 0)),
            pl.BlockSpec((8, tm, LANES), lambda i: (0, i, 0)),
            pl.BlockSpec((tm, D_MODEL), lambda i: (i, 0)),
            pl.BlockSpec((D_MODEL, D_MODEL), lambda i: (0, 0), pipeline_mode=pl.Buffered(1)),
            pl.BlockSpec((1, D_MODEL), lambda i: (0, 0)),
        ],
        out_specs=pl.BlockSpec((tm, D_MODEL), lambda i: (i, 0)),
        out_shape=jax.ShapeDtypeStruct((T, D_MODEL), F32),
        compiler_params=pltpu.CompilerParams(
            dimension_semantics=("parallel",),
            vmem_limit_bytes=VMEM_LIMIT),
        name="out_proj",
    )(o_da, o_sw, h, w_out, g)


def _layer(x, l, wts):
    batch, s_len, _ = x.shape
    assert s_len % (DA_TQ * DA_TILES) == 0 and (s_len // DA_TQ) % DA_PV_GROUP == 0
    assert s_len // DA_TQ > 3 and s_len % SW_TQ == 0 and s_len >= SW_TQ + 2 * WINDOW
    assert (batch * s_len) % FFN_TM == 0
    lambda_init = 0.8 - 0.6 * math.exp(-0.3 * l)
    x2 = x.reshape(batch * s_len, D_MODEL)
    h = _ffn(x2, wts["g_ffn1_pre"], wts["w_ffn1_gu"], wts["w_ffn1_down"], wts["g_ffn1_post_half"])
    proj = _in_proj(h, wts["g_mix_pre"], wts["w_in"])
    o_da = _diff_attn(proj, wts["da_bias"], wts["da_far"], wts["lambda_q1"], wts["lambda_k1"],
                      wts["lambda_q2"], wts["lambda_k2"], wts["g_diff_subln"], batch, s_len, lambda_init)
    o_sw = _win_attn(proj, wts["sw_bias"], wts["sink_log2"], batch, s_len)
    h2 = _out_proj(o_da, o_sw, h, wts["w_out"], wts["g_mix_post"])
    y = _ffn(h2, wts["g_ffn2_pre"], wts["w_ffn2_gu"], wts["w_ffn2_down"], wts["g_ffn2_post_half"])
    return y.reshape(batch, s_len, D_MODEL)


def kernel(x_prompt, x_sample, rel_bias, g_ffn1_pre, w_ffn1_gu, w_ffn1_down, g_ffn1_post, g_mix_pre, w_in, lambda_q1, lambda_k1, lambda_q2, lambda_k2, g_diff_subln, sink_logit, w_out, g_mix_post, g_ffn2_pre, w_ffn2_gu, w_ffn2_down, g_ffn2_post):
    depth = w_in.shape[0]
    layers = []
    for l in range(depth):
        layers.append({
            "g_ffn1_pre": g_ffn1_pre[l][None], "g_ffn1_post_half": 0.5 * g_ffn1_post[l][None],
            "w_ffn1_gu": w_ffn1_gu[l].astype(BF16), "w_ffn1_down": w_ffn1_down[l].astype(BF16),
            "g_mix_pre": g_mix_pre[l][None], "w_in": w_in[l].astype(BF16),
            "lambda_q1": lambda_q1[l][None], "lambda_k1": lambda_k1[l][None],
            "lambda_q2": lambda_q2[l][None], "lambda_k2": lambda_k2[l][None],
            "g_diff_subln": g_diff_subln[l][None], "sink_log2": sink_logit[l] * LOG2E,
            "w_out": w_out[l].astype(BF16), "g_mix_post": g_mix_post[l][None],
            "g_ffn2_pre": g_ffn2_pre[l][None], "g_ffn2_post_half": 0.5 * g_ffn2_post[l][None],
            "w_ffn2_gu": w_ffn2_gu[l].astype(BF16), "w_ffn2_down": w_ffn2_down[l].astype(BF16),
            "da_bias": _da_bias_table(rel_bias), "da_far": _da_far_bias(rel_bias),
            "sw_bias": _sw_bias_table(rel_bias, SW_TQ),
        })

    def trunk(x):
        for l in range(depth):
            x = _layer(x, l, layers[l])
        return x

    return (trunk(x_prompt), trunk(x_sample))
```

```python
import functools
import math

import jax
import jax.numpy as jnp
from jax import lax
from jax.experimental import pallas as pl
from jax.experimental.pallas import tpu as pltpu

F32 = jnp.float32
BF16 = jnp.bfloat16

D_MODEL = 2048
D_FF = 5632
DA_HEADS = 8
DA_DK = 64
DA_DV = 128
SW_Q_HEADS = 16
SW_KV_HEADS = 4
SW_DH = 64
WINDOW = 128
N_BUCKETS = 32
MAX_DISTANCE = 128
EPS = 1e-6
D_IN = 4608
LANES = 128
N_SLABS = D_IN // LANES
LOG2E = math.log2(math.e)
Q_SCALE = 0.125 * LOG2E
NEG_BIG = -1e30

FFN_TM = 1024
FFN_TF = 512
FFN_ROW_SPLIT = 2
NORM_ROWS = 16
PROJ_TM = 512
PROJ_TN = 512
DA_TQ = 256
DA_SCORE_BYTES = 32 * 1024 * 1024
DA_QK_CHUNKS = 4
DA_VROWS = DA_DV + 16
SW_TQ = 128
SW_SUBTILES = 8
VMEM_LIMIT = 56 * 1024 * 1024
FFN_VMEM_LIMIT = 62 * 1024 * 1024


def _rms(x, g):
    return x * lax.rsqrt(jnp.mean(x * x, axis=-1, keepdims=True) + EPS) * g


def _ffn_kernel(x_ref, gpre_ref, wg_ref, wu_ref, wd_ref, ghalf_ref, o_ref, xn_ref):
    j = pl.program_id(1)
    last = pl.num_programs(1) - 1
    rows = xn_ref.shape[0] // FFN_ROW_SPLIT

    def step(first, final):
        for r in range(FFN_ROW_SPLIT):
            blocks = [slice(r * rows + i * NORM_ROWS, r * rows + (i + 1) * NORM_ROWS)
                      for i in range(rows // NORM_ROWS)]
            rs = slice(r * rows, (r + 1) * rows)
            if first:
                for b in blocks:
                    xn_ref[b, :] = _rms(x_ref[b, :], gpre_ref[...]).astype(BF16)
            xn = xn_ref[rs, :]
            g = jnp.dot(xn, wg_ref[...], preferred_element_type=F32)
            u = jnp.dot(xn, wu_ref[...], preferred_element_type=F32)
            a = (g / (1.0 + jnp.exp(-g)) * u).astype(BF16)
            d = jnp.dot(a, wd_ref[...], preferred_element_type=F32)
            if first:
                o_ref[rs, :] = d
            else:
                o_ref[rs, :] += d
            if final:
                for b in blocks:
                    o_ref[b, :] = x_ref[b, :] + _rms(o_ref[b, :], ghalf_ref[...])

    pl.when(j == 0)(functools.partial(step, True, False))
    pl.when(jnp.logical_and(j > 0, j < last))(functools.partial(step, False, False))
    pl.when(j == last)(functools.partial(step, False, True))


def _ffn(x, g_pre, w_gu, w_down, g_post_half):
    T = x.shape[0]
    tm, tf = FFN_TM, FFN_TF
    nf = D_FF // tf
    return pl.pallas_call(
        _ffn_kernel,
        grid=(T // tm, nf),
        in_specs=[
            pl.BlockSpec((tm, D_MODEL), lambda i, j: (i, 0)),
            pl.BlockSpec((1, D_MODEL), lambda i, j: (0, 0)),
            pl.BlockSpec((D_MODEL, tf), lambda i, j: (0, j)),
            pl.BlockSpec((D_MODEL, tf), lambda i, j: (0, j + nf)),
            pl.BlockSpec((tf, D_MODEL), lambda i, j: (j, 0)),
            pl.BlockSpec((1, D_MODEL), lambda i, j: (0, 0)),
        ],
        out_specs=pl.BlockSpec((tm, D_MODEL), lambda i, j: (i, 0)),
        out_shape=jax.ShapeDtypeStruct((T, D_MODEL), F32),
        scratch_shapes=[pltpu.VMEM((tm, D_MODEL), BF16)],
        compiler_params=pltpu.CompilerParams(
            dimension_semantics=("parallel", "arbitrary"),
            vmem_limit_bytes=FFN_VMEM_LIMIT),
        name="ffn",
    )(x, g_pre, w_gu, w_gu, w_down, g_post_half)


def _proj_kernel(h_ref, g_ref, w_ref, o_ref):
    n = _rms(h_ref[...], g_ref[...]).astype(BF16)
    per = PROJ_TN // LANES
    for c in range(D_IN // PROJ_TN):
        r = jnp.dot(n, w_ref[:, c * PROJ_TN:(c + 1) * PROJ_TN], preferred_element_type=F32)
        for s in range(per):
            slab = c * per + s
            blk = r[:, s * LANES:(s + 1) * LANES]
            if slab < 8 or 24 <= slab < 32:
                blk = blk * Q_SCALE
            o_ref[slab] = blk.astype(BF16)


def _in_proj(h, g, w_in):
    T = h.shape[0]
    return pl.pallas_call(
        _proj_kernel,
        grid=(T // PROJ_TM,),
        in_specs=[
            pl.BlockSpec((PROJ_TM, D_MODEL), lambda i: (i, 0)),
            pl.BlockSpec((1, D_MODEL), lambda i: (0, 0)),
            pl.BlockSpec((D_MODEL, D_IN), lambda i: (0, 0), pipeline_mode=pl.Buffered(1)),
        ],
        out_specs=pl.BlockSpec((N_SLABS, PROJ_TM, LANES), lambda i: (0, i, 0)),
        out_shape=jax.ShapeDtypeStruct((N_SLABS, T, LANES), BF16),
        compiler_params=pltpu.CompilerParams(
            dimension_semantics=("parallel",),
            vmem_limit_bytes=VMEM_LIMIT),
        name="in_proj",
    )(h, g, w_in)


def _t5_bucket(rp):
    half = N_BUCKETS // 2
    max_exact = half // 2
    ret = jnp.where(rp > 0, half, 0)
    n = jnp.abs(rp)
    nf = jnp.maximum(n, 1).astype(jnp.float32)
    large = max_exact + (jnp.log(nf / max_exact) / math.log(MAX_DISTANCE / max_exact)
                         * (half - max_exact)).astype(jnp.int32)
    large = jnp.minimum(large, half - 1)
    return ret + jnp.where(n < max_exact, n, large)


def _bias_lookup(table, rp):
    bucket = _t5_bucket(rp)[None]
    col = lambda b: (table[b].astype(F32) * LOG2E).reshape((-1,) + (1,) * rp.ndim)
    out = jnp.broadcast_to(col(0), (table.shape[1],) + rp.shape)
    for b in range(1, N_BUCKETS):
        out = jnp.where(bucket == b, col(b), out)
    return out


DA_BAND_OFFSETS = (-WINDOW, 0, WINDOW, 2 * WINDOW)
DA_IDX_LEFT = len(DA_BAND_OFFSETS)
DA_IDX_RIGHT = DA_IDX_LEFT + 1
FAR = 2 * (DA_TQ + 2 * WINDOW)


def _da_bias_table(rel_bias):
    offs = jnp.array(DA_BAND_OFFSETS + (-FAR, FAR))
    rp = offs[:, None, None] + jnp.arange(LANES)[None, :, None] - jnp.arange(DA_TQ)[None, None, :]
    return _bias_lookup(rel_bias[:, :DA_HEADS], rp)


def _da_far_bias(rel_bias):
    return _bias_lookup(rel_bias[:, :DA_HEADS], jnp.array([FAR, -FAR]))


def _sw_bias_table(rel_bias, tq):
    wk = tq + 2 * WINDOW
    starts = jnp.array([0, -WINDOW, -2 * WINDOW])
    rp = starts[:, None, None] + jnp.arange(wk)[None, None, :] - jnp.arange(tq)[None, :, None]
    tab = _bias_lookup(rel_bias[:, DA_HEADS:], rp)
    return jnp.where((jnp.abs(rp) <= WINDOW)[None], tab, NEG_BIG)


def _da_kernel(far_ref, lq1_ref, lk1_ref, lq2_ref, lk2_ref, q_ref, k_ref, v_ref, bias_ref, g_ref,
               o_ref, s_ref, krot_ref, vt_ref, *, s_len, lambda_init):
    tq, ntile = DA_TQ, _da_tiles(s_len)
    ck = tq
    nck = s_len // ck
    h = pl.program_id(0)
    t = pl.program_id(2)

    @pl.when(t == 0)
    def _():
        for c in range(nck):
            vt_ref[c, :DA_DV, :] = v_ref[0, c * ck:(c + 1) * ck, :].astype(F32).T.astype(BF16)
            vt_ref[c, DA_DV:, :] = jnp.ones((DA_VROWS - DA_DV, ck), BF16)

    c_right = far_ref[h, 0]
    c_left = far_ref[h, 1]
    lo = lax.broadcasted_iota(jnp.int32, (tq, LANES), 1) < DA_DK
    lam = (jnp.exp(jnp.sum(lq1_ref[...] * lk1_ref[...], axis=1, keepdims=True))
           - jnp.exp(jnp.sum(lq2_ref[...] * lk2_ref[...], axis=1, keepdims=True))
           + lambda_init)

    def chunks(i):
        q0 = (t * ntile + i) * tq
        out = []
        for r in range(nck):
            u = q0 + (r - 1) * ck
            wrapped = (u < 0) if r == 0 else (u >= s_len)
            k0 = jnp.where(wrapped, u + s_len if r == 0 else u - s_len, u)
            out.append((pl.multiple_of(k0, ck), wrapped))
        return out

    def fold(x):
        return jnp.max(x.reshape(x.shape[0] // 8, 8, x.shape[1]), axis=0)

    def scores(i, ch):
        q = q_ref[0, i * tq:(i + 1) * tq, :]
        zero = jnp.zeros_like(q)
        qs = jnp.concatenate([jnp.where(lo, q, zero), jnp.where(lo, zero, q)], axis=0)
        for r in range(nck):
            krot_ref[i, r * ck:(r + 1) * ck, :] = k_ref[0, pl.ds(ch[r][0], ck), :]
        w0, w2 = ch[0][1], ch[2][1]
        near = [
            (jnp.where(w0, DA_IDX_RIGHT, DA_IDX_LEFT), jnp.where(w0, DA_IDX_RIGHT, 0)),
            (1, 2),
            (jnp.where(w2, DA_IDX_LEFT, 3), jnp.where(w2, DA_IDX_LEFT, DA_IDX_RIGHT)),
        ]
        mx = jnp.full((8, 2 * tq), -jnp.inf, F32)
        for g0 in range(0, nck, DA_QK_CHUNKS):
            st = lax.dot_general(krot_ref[i, g0 * ck:(g0 + DA_QK_CHUNKS) * ck, :], qs,
                                 (((1,), (1,)), ((), ())), preferred_element_type=F32)
            for r in range(g0, g0 + DA_QK_CHUNKS):
                blk = st[(r - g0) * ck:(r - g0 + 1) * ck]
                if r < 3:
                    halves = []
                    for jb in range(2):
                        tile = bias_ref[0, near[r][jb]]
                        halves.append(blk[jb * LANES:(jb + 1) * LANES] + jnp.concatenate([tile, tile], axis=1))
                    blk = jnp.concatenate(halves, axis=0)
                    mx = jnp.maximum(mx, fold(blk))
                else:
                    mx = jnp.maximum(mx, fold(blk) + jnp.where(ch[r][1], c_left, c_right))
                s_ref[i, r * ck:(r + 1) * ck, :] = blk
        return jnp.max(mx, axis=0, keepdims=True)

    def attend(i, ch, m):
        off_right = m - c_right
        off_left = m - c_left
        acc = jnp.zeros((DA_VROWS, 2 * tq), F32)
        for r in range(nck):
            off = m if r < 3 else jnp.where(ch[r][1], off_left, off_right)
            pt = jnp.exp2(s_ref[i, r * ck:(r + 1) * ck, :] - off).astype(BF16)
            acc = acc + jnp.dot(vt_ref[ch[r][0] // ck], pt, preferred_element_type=F32)
        o = acc[:DA_DV] / acc[DA_DV:DA_DV + 1]
        o = (o[:, :tq] - lam * o[:, tq:]).T
        o_ref[0, i * tq:(i + 1) * tq, :] = (_rms(o, g_ref[...]) * (1.0 - lambda_init)).astype(BF16)

    chs = [chunks(i) for i in range(ntile)]
    ms = [scores(i, chs[i]) for i in range(ntile)]
    for i in range(ntile):
        attend(i, chs[i], ms[i])


def _da_tiles(s_len):
    return min(s_len // DA_TQ, DA_SCORE_BYTES // (2 * DA_TQ * s_len * 4))


def _diff_attn(proj, bias_tab, far_bias, lq1, lk1, lq2, lk2, g_subln, batch, s_len, lambda_init):
    T = batch * s_len
    ntile = _da_tiles(s_len)
    rows = DA_TQ * ntile
    nt = s_len // rows
    ntab = bias_tab.shape[1]
    vec = pl.BlockSpec((1, DA_DK), lambda h, b, t: (0, 0))
    return pl.pallas_call(
        functools.partial(_da_kernel, s_len=s_len, lambda_init=lambda_init),
        grid=(DA_HEADS, batch, nt),
        in_specs=[
            pl.BlockSpec(memory_space=pltpu.SMEM),
            vec, vec, vec, vec,
            pl.BlockSpec((1, rows, LANES), lambda h, b, t: (h, b * nt + t, 0)),
            pl.BlockSpec((1, s_len, LANES), lambda h, b, t: (8 + h, b, 0)),
            pl.BlockSpec((1, s_len, LANES), lambda h, b, t: (16 + h, b, 0)),
            pl.BlockSpec((1, ntab, LANES, DA_TQ), lambda h, b, t: (h, 0, 0, 0)),
            pl.BlockSpec((1, DA_DV), lambda h, b, t: (0, 0)),
        ],
        out_specs=pl.BlockSpec((1, rows, LANES), lambda h, b, t: (h, b * nt + t, 0)),
        out_shape=jax.ShapeDtypeStruct((DA_HEADS, T, LANES), BF16),
        scratch_shapes=[pltpu.VMEM((ntile, s_len, 2 * DA_TQ), F32),
                        pltpu.VMEM((ntile, s_len, LANES), BF16),
                        pltpu.VMEM((s_len // DA_TQ, DA_VROWS, DA_TQ), BF16)],
        compiler_params=pltpu.CompilerParams(
            dimension_semantics=("parallel", "parallel", "arbitrary"),
            vmem_limit_bytes=VMEM_LIMIT),
        name="diff_attn",
    )(far_bias, lq1, lk1, lq2, lk2, proj, proj, proj, bias_tab, g_subln)


def _swap_halves(x):
    return jnp.concatenate([x[:, SW_DH:], x[:, :SW_DH]], axis=1)


def _sw_kernel(sink_ref, q_ref, k_ref, v_ref, bias_ref, o_ref, vext_ref, *, s_len):
    tq, nsub = SW_TQ, SW_SUBTILES
    wk = tq + 2 * WINDOW
    per = wk // LANES
    p = pl.program_id(0)
    t = pl.program_id(2)

    @pl.when(t == 0)
    def _():
        vext_ref[:, :LANES] = v_ref[0]
        vext_ref[:, LANES:] = jnp.ones((s_len, LANES), BF16)

    lo = lax.broadcasted_iota(jnp.int32, (tq, LANES), 1) < SW_DH
    sinks = [sink_ref[p * 8 + j] for j in range(8)]

    def window(sub):
        q0 = (t * nsub + sub) * tq
        start = pl.multiple_of(jnp.clip(q0 - WINDOW, 0, s_len - wk), LANES)
        case = jnp.where(q0 == 0, 0, jnp.where(q0 == s_len - tq, 2, 1))
        return start, case

    def scores(sub, kv_half):
        start, case = window(sub)
        heads = [4 * kv_half + jj for jj in range(4)]
        lhs = []
        for j in heads:
            xq = q_ref[j // 2, sub * tq:(sub + 1) * tq, :]
            if j % 2 != kv_half:
                xq = _swap_halves(xq)
            zero = jnp.zeros_like(xq)
            lhs.append(jnp.where(lo, xq, zero) if kv_half == 0 else jnp.where(lo, zero, xq))
        s = lax.dot_general(jnp.concatenate(lhs, axis=0), k_ref[0, pl.ds(start, wk), :],
                            (((1,), (1,)), ((), ())), preferred_element_type=F32)
        bias = jnp.concatenate([bias_ref[j, case] for j in heads], axis=0)
        sb = [s[:, jb * LANES:(jb + 1) * LANES] + bias[:, jb * LANES:(jb + 1) * LANES]
              for jb in range(per)]
        mx = sb[0]
        for blk in sb[1:]:
            mx = jnp.maximum(mx, blk)
        sk = jnp.concatenate([jnp.full((tq, LANES), sinks[j], F32) for j in heads], axis=0)
        m = jnp.maximum(jnp.broadcast_to(jnp.max(mx, axis=1, keepdims=True), sk.shape), sk)
        return sb, m, sk

    def attend(sub, kv_half, sb, m, sk):
        start, _ = window(sub)
        e = jnp.concatenate([jnp.exp2(blk - m).astype(BF16) for blk in sb], axis=1)
        r = jnp.dot(e, vext_ref[pl.ds(start, wk), :], preferred_element_type=F32)
        o = r[:, :LANES] / (r[:, LANES:] + jnp.exp2(sk - m))
        outs = []
        for jj in range(4):
            oj = o[jj * tq:(jj + 1) * tq]
            outs.append(_swap_halves(oj) if jj % 2 != kv_half else oj)
        return outs

    units = [(sub, kv_half) for sub in range(nsub) for kv_half in range(2)]
    staged = [scores(*u) for u in units]
    outs = [attend(*u, *st) for u, st in zip(units, staged)]
    for sub in range(nsub):
        heads = outs[2 * sub] + outs[2 * sub + 1]
        for ob in range(4):
            o_ref[ob, sub * tq:(sub + 1) * tq, :] = jnp.where(lo, heads[2 * ob], heads[2 * ob + 1]).astype(BF16)


def _win_attn(proj, bias_tab, sink, batch, s_len):
    T = batch * s_len
    rows = SW_TQ * SW_SUBTILES
    nt = s_len // rows
    wk = SW_TQ + 2 * WINDOW
    return pl.pallas_call(
        functools.partial(_sw_kernel, s_len=s_len),
        grid=(2, batch, nt),
        in_specs=[
            pl.BlockSpec(memory_space=pltpu.SMEM),
            pl.BlockSpec((4, rows, LANES), lambda p, b, t: (6 + p, b * nt + t, 0)),
            pl.BlockSpec((1, s_len, LANES), lambda p, b, t: (32 + p, b, 0)),
            pl.BlockSpec((1, s_len, LANES), lambda p, b, t: (34 + p, b, 0)),
            pl.BlockSpec((8, 3, SW_TQ, wk), lambda p, b, t: (p, 0, 0, 0)),
        ],
        out_specs=pl.BlockSpec((4, rows, LANES), lambda p, b, t: (p, b * nt + t, 0)),
        out_shape=jax.ShapeDtypeStruct((SW_Q_HEADS // 2, T, LANES), BF16),
        scratch_shapes=[pltpu.VMEM((s_len, 2 * LANES), BF16)],
        compiler_params=pltpu.CompilerParams(
            dimension_semantics=("parallel", "parallel", "arbitrary"),
            vmem_limit_bytes=VMEM_LIMIT),
        name="win_attn",
    )(sink, proj, proj, proj, bias_tab)


def _out_kernel(da_ref, sw_ref, h_ref, w_ref, g_ref, o_ref):
    a = jnp.concatenate([da_ref[c] for c in range(8)] + [sw_ref[c] for c in range(8)], axis=1)
    mix = jnp.dot(a, w_ref[...], preferred_element_type=F32)
    o_ref[...] = h_ref[...] + _rms(mix, g_ref[...])


def _out_proj(o_da, o_sw, h, w_out, g):
    T = h.shape[0]
    tm = PROJ_TM
    return pl.pallas_call(
        _out_kernel,
        grid=(T // tm,),
        in_specs=[
            pl.BlockSpec((8, tm, LANES), lambda i: (0, i, 0)),
            pl.BlockSpec((8, tm, LANES), lambda i: (0, i, 0)),
            pl.BlockSpec((tm, D_MODEL), lambda i: (i, 0)),
            pl.BlockSpec((D_MODEL, D_MODEL), lambda i: (0, 0), pipeline_mode=pl.Buffered(1)),
            pl.BlockSpec((1, D_MODEL), lambda i: (0, 0)),
        ],
        out_specs=pl.BlockSpec((tm, D_MODEL), lambda i: (i, 0)),
        out_shape=jax.ShapeDtypeStruct((T, D_MODEL), F32),
        compiler_params=pltpu.CompilerParams(
            dimension_semantics=("parallel",),
            vmem_limit_bytes=VMEM_LIMIT),
        name="out_proj",
    )(o_da, o_sw, h, w_out, g)


def _layer(x, l, wts):
    batch, s_len, _ = x.shape
    assert s_len % (DA_TQ * _da_tiles(s_len)) == 0 and (s_len // DA_TQ) % DA_QK_CHUNKS == 0
    assert s_len // DA_TQ > 3 and s_len % (SW_TQ * SW_SUBTILES) == 0 and s_len >= 2 * SW_TQ + 2 * WINDOW
    assert (batch * s_len) % FFN_TM == 0
    lambda_init = 0.8 - 0.6 * math.exp(-0.3 * l)
    x2 = x.reshape(batch * s_len, D_MODEL)
    h = _ffn(x2, wts["g_ffn1_pre"], wts["w_ffn1_gu"], wts["w_ffn1_down"], wts["g_ffn1_post_half"])
    proj = _in_proj(h, wts["g_mix_pre"], wts["w_in"])
    o_da = _diff_attn(proj, wts["da_bias"], wts["da_far"], wts["lambda_q1"], wts["lambda_k1"],
                      wts["lambda_q2"], wts["lambda_k2"], wts["g_diff_subln"], batch, s_len, lambda_init)
    o_sw = _win_attn(proj, wts["sw_bias"], wts["sink_log2"], batch, s_len)
    h2 = _out_proj(o_da, o_sw, h, wts["w_out"], wts["g_mix_post"])
    y = _ffn(h2, wts["g_ffn2_pre"], wts["w_ffn2_gu"], wts["w_ffn2_down"], wts["g_ffn2_post_half"])
    return y.reshape(batch, s_len, D_MODEL)


def kernel(x_prompt, x_sample, rel_bias, g_ffn1_pre, w_ffn1_gu, w_ffn1_down, g_ffn1_post, g_mix_pre, w_in, lambda_q1, lambda_k1, lambda_q2, lambda_k2, g_diff_subln, sink_logit, w_out, g_mix_post, g_ffn2_pre, w_ffn2_gu, w_ffn2_down, g_ffn2_post):
    depth = w_in.shape[0]
    layers = []
    for l in range(depth):
        layers.append({
            "g_ffn1_pre": g_ffn1_pre[l][None], "g_ffn1_post_half": 0.5 * g_ffn1_post[l][None],
            "w_ffn1_gu": w_ffn1_gu[l].astype(BF16), "w_ffn1_down": w_ffn1_down[l].astype(BF16),
            "g_mix_pre": g_mix_pre[l][None], "w_in": w_in[l].astype(BF16),
            "lambda_q1": lambda_q1[l][None], "lambda_k1": lambda_k1[l][None],
            "lambda_q2": lambda_q2[l][None], "lambda_k2": lambda_k2[l][None],
            "g_diff_subln": g_diff_subln[l][None], "sink_log2": sink_logit[l] * LOG2E,
            "w_out": w_out[l].astype(BF16), "g_mix_post": g_mix_post[l][None],
            "g_ffn2_pre": g_ffn2_pre[l][None], "g_ffn2_post_half": 0.5 * g_ffn2_post[l][None],
            "w_ffn2_gu": w_ffn2_gu[l].astype(BF16), "w_ffn2_down": w_ffn2_down[l].astype(BF16),
            "da_bias": _da_bias_table(rel_bias), "da_far": _da_far_bias(rel_bias),
            "sw_bias": _sw_bias_table(rel_bias, SW_TQ),
        })

    def trunk(x):
        for l in range(depth):
            x = _layer(x, l, layers[l])
        return x

    return (trunk(x_prompt), trunk(x_sample))
```
